```python
import math
import jax, jax.numpy as jnp
from jax import lax
import numpy as np

D_MODEL = 1024
BATCH = 8
SEQ = 4096
DEPTH = 1

EPS = 1e-6
SSD_HEAD_DIM = 64
D_SSD = D_MODEL
SSD_HEADS = D_SSD // SSD_HEAD_DIM
SSD_GROUPS = 2
SSD_STATE = 128
CONV_K = 4
SSD_CHUNK = 128
CONV_CH = D_SSD + 2 * SSD_GROUPS * SSD_STATE
ATTN_HEAD_DIM = 64
D_ATTN = D_MODEL
ATTN_Q_HEADS = D_ATTN // ATTN_HEAD_DIM
ATTN_KV_HEADS = 2
Q_PER_KV = ATTN_Q_HEADS // ATTN_KV_HEADS
D_KV = ATTN_KV_HEADS * ATTN_HEAD_DIM
WINDOW = 128
ATTN_SCALE = ATTN_HEAD_DIM ** -0.5
D_MIX = D_SSD + D_ATTN
SPLITS = (D_SSD,
          D_SSD + CONV_CH,
          D_SSD + CONV_CH + SSD_HEADS,
          D_SSD + CONV_CH + SSD_HEADS + D_ATTN,
          D_SSD + CONV_CH + SSD_HEADS + D_ATTN + D_KV)
IN_DIM = D_SSD + CONV_CH + SSD_HEADS + D_ATTN + 2 * D_KV
MOE_GROUPS = 8
EXPERTS_PER_GROUP = 8
N_EXPERTS = MOE_GROUPS * EXPERTS_PER_GROUP
TOP_K = 2
D_FF_EXPERT = D_MODEL // 2
MOE_BLOCK = 128

kernel_name = "hymba_ssd_swa_sink_hmoe_block"


def rmsnorm(x, g):
    xf = x.astype(jnp.float32)
    y = xf * lax.rsqrt(jnp.mean(xf * xf, axis=-1, keepdims=True) + EPS)
    return (y * g.astype(jnp.float32)).astype(x.dtype)


def causal_depthwise_conv(u, w, b):
    c = u.shape[-1]
    y = lax.conv_general_dilated(u, w[:, None, :], window_strides=(1,),
                                 padding=[(CONV_K - 1, 0)],
                                 dimension_numbers=("NWC", "WIO", "NWC"),
                                 feature_group_count=c)
    return y + b


def ssd_chunked(x, dt, A, B, C):
    b, s, h, p = x.shape
    g, n = B.shape[2], B.shape[3]
    r = h // g
    c = s // SSD_CHUNK
    L = SSD_CHUNK
    xc = x.reshape(b, c, L, g, r, p)
    dtc = dt.reshape(b, c, L, g, r)
    Bc = B.reshape(b, c, L, g, n)
    Cc = C.reshape(b, c, L, g, n)
    a = jnp.moveaxis(dtc * A.reshape(g, r), 2, -1)
    a_cs = jnp.cumsum(a, axis=-1)
    xdt = xc * dtc[..., None]
    idx = jnp.arange(L)
    causal = idx[:, None] >= idx[None, :]
    seg = a_cs[..., :, None] - a_cs[..., None, :]
    decay = jnp.exp(jnp.where(causal, seg, -jnp.inf))
    cb = jnp.einsum("bclgn,bcsgn->bcgls", Cc, Bc)
    y_diag = jnp.einsum("bcgls,bcgrls,bcsgrp->bclgrp", cb, decay, xdt)
    decay_to_end = jnp.exp(a_cs[..., -1:] - a_cs)
    states = jnp.einsum("bclgn,bcgrl,bclgrp->bcgrpn", Bc, decay_to_end, xdt)
    chunk_decay = jnp.exp(a_cs[..., -1])

    def step(carry, inp):
        st, dec = inp
        return carry * dec[..., None, None] + st, carry

    init = jnp.zeros((b, g, r, p, n), states.dtype)
    _, prev = lax.scan(step, init, (jnp.moveaxis(states, 1, 0), jnp.moveaxis(chunk_decay, 1, 0)))
    prev = jnp.moveaxis(prev, 0, 1)
    y_off = jnp.einsum("bclgn,bcgrpn,bcgrl->bclgrp", Cc, prev, jnp.exp(a_cs))
    return (y_diag + y_off).reshape(b, s, h, p)


def gated_group_rmsnorm(y, z, gain):
    u = y * jax.nn.silu(z.astype(jnp.float32))
    u = u.reshape(*y.shape[:-1], SSD_GROUPS, -1)
    u = u * lax.rsqrt(jnp.mean(u * u, axis=-1, keepdims=True) + EPS)
    return u.reshape(y.shape) * gain.astype(jnp.float32)


def swa_gqa_sinks(q, k, v, sinks):
    b, s = q.shape[0], q.shape[1]
    nb = s // WINDOW
    W = WINDOW
    qb = q.astype(jnp.float32).reshape(b, nb, W, ATTN_KV_HEADS, Q_PER_KV, ATTN_HEAD_DIM) * ATTN_SCALE
    kb = k.astype(jnp.float32).reshape(b, nb, W, ATTN_KV_HEADS, ATTN_HEAD_DIM)
    vb = v.astype(jnp.float32).reshape(b, nb, W, ATTN_KV_HEADS, ATTN_HEAD_DIM)
    pad = ((0, 0), (1, 0), (0, 0), (0, 0), (0, 0))
    kcat = jnp.concatenate([jnp.pad(kb, pad)[:, :-1], kb], axis=2)
    vcat = jnp.concatenate([jnp.pad(vb, pad)[:, :-1], vb], axis=2)
    scores = jnp.einsum("bnqhrd,bnkhd->bnhrqk", qb, kcat)
    qi = jnp.arange(W)[:, None]
    kj = jnp.arange(2 * W)[None, :]
    rel = qi + W - kj
    band = (rel >= 0) & (rel < WINDOW)
    blk = jnp.arange(nb)[:, None, None]
    valid = band[None] & ((blk > 0) | (kj >= W)[None])
    scores = jnp.where(valid[None, :, None, None], scores, -jnp.inf)
    sink = sinks.astype(jnp.float32).reshape(ATTN_KV_HEADS, Q_PER_KV)[None, None, :, :, None, None]
    m = jnp.maximum(jnp.max(scores, axis=-1, keepdims=True), sink)
    pr = jnp.exp(scores - m)
    denom = jnp.sum(pr, axis=-1, keepdims=True) + jnp.exp(sink - m)
    out = jnp.einsum("bnhrqk,bnkhd->bnqhrd", pr / denom, vcat)
    return out.reshape(b, s, ATTN_Q_HEADS * ATTN_HEAD_DIM)


def hybrid_mixer(h, w_in, conv_w, conv_b, dt_bias, a_log, d_skip, ssd_norm,
                 attn_sinks, attn_norm, w_out):
    b, s, _ = h.shape
    proj = h @ w_in
    z, xbc, dt_raw, q, k, v = jnp.split(proj, SPLITS, axis=-1)
    xbc = jax.nn.silu(causal_depthwise_conv(xbc, conv_w, conv_b)).astype(jnp.float32)
    xs, Bm, Cm = jnp.split(xbc, (D_SSD, D_SSD + SSD_GROUPS * SSD_STATE), axis=-1)
    xs = xs.reshape(b, s, SSD_HEADS, SSD_HEAD_DIM)
    dt = jax.nn.softplus(dt_raw.astype(jnp.float32) + dt_bias.astype(jnp.float32))
    A = -jnp.exp(a_log.astype(jnp.float32))
    y = ssd_chunked(xs, dt, A,
                    Bm.reshape(b, s, SSD_GROUPS, SSD_STATE),
                    Cm.reshape(b, s, SSD_GROUPS, SSD_STATE))
    y = y + xs * d_skip.astype(jnp.float32)[:, None]
    y_ssd = gated_group_rmsnorm(y.reshape(b, s, D_SSD), z, ssd_norm)
    att = swa_gqa_sinks(q.reshape(b, s, ATTN_Q_HEADS, ATTN_HEAD_DIM),
                        k.reshape(b, s, ATTN_KV_HEADS, ATTN_HEAD_DIM),
                        v.reshape(b, s, ATTN_KV_HEADS, ATTN_HEAD_DIM), attn_sinks)
    y_att = rmsnorm(att, attn_norm)
    mixed = jnp.concatenate([y_ssd, y_att], axis=-1).astype(h.dtype)
    return mixed @ w_out


def hierarchical_moe(h, w_router_group, w_router_expert, w_gate, w_up, w_down):
    b, s, d = h.shape
    t = h.reshape(-1, d)
    n = t.shape[0]
    rows = jnp.arange(n)
    group_logits = (t @ w_router_group).astype(jnp.float32)
    group_prob = jax.nn.softmax(group_logits, axis=-1)
    g_sel = jnp.argmax(group_logits, axis=-1)
    p_group = group_prob[rows, g_sel][:, None]
    exp_logits = (t @ w_router_expert).astype(jnp.float32).reshape(n, MOE_GROUPS, EXPERTS_PER_GROUP)
    in_group = exp_logits[rows, g_sel]
    top_logit, top_local = lax.top_k(in_group, TOP_K)
    gate = jax.nn.softmax(top_logit, axis=-1) * p_group
    expert_id = g_sel[:, None] * EXPERTS_PER_GROUP + top_local
    a = n * TOP_K
    flat_e = expert_id.reshape(-1)
    flat_g = gate.reshape(-1)
    order = jnp.argsort(flat_e)
    e_sorted = flat_e[order]
    tok_sorted = order // TOP_K
    counts = jnp.bincount(flat_e, length=N_EXPERTS)
    padded = ((counts + MOE_BLOCK - 1) // MOE_BLOCK) * MOE_BLOCK
    start = jnp.cumsum(counts) - counts
    pend = jnp.cumsum(padded)
    pstart = pend - padded
    dest = pstart[e_sorted] + jnp.arange(a) - start[e_sorted]
    n_blocks = -(-(a + N_EXPERTS * (MOE_BLOCK - 1)) // MOE_BLOCK)
    buf = jnp.zeros((n_blocks * MOE_BLOCK, d), t.dtype).at[dest].set(t[tok_sorted])
    block_expert = jnp.minimum(
        jnp.searchsorted(pend, jnp.arange(n_blocks) * MOE_BLOCK, side="right"), N_EXPERTS - 1)

    def expert_block(args):
        xb, e = args
        hid = jax.nn.silu(xb @ w_gate[e]) * (xb @ w_up[e])
        return hid @ w_down[e]

    y_buf = lax.map(expert_block, (buf.reshape(n_blocks, MOE_BLOCK, d), block_expert))
    y_buf = y_buf.reshape(-1, d)
    y = jnp.zeros_like(t).at[tok_sorted].add(y_buf[dest] * flat_g[order][:, None].astype(t.dtype))
    return y.reshape(b, s, d)


def setup_inputs(seed: int = 0) -> dict:
    key = jax.random.key(seed)
    ks = jax.random.split(key, 20)
    f32 = jnp.float32
    nrm = lambda k, shp, sc: jax.random.normal(k, shp, f32) * sc
    dt0 = jnp.exp(jax.random.uniform(ks[5], (DEPTH, SSD_HEADS), f32)
                  * (math.log(0.1) - math.log(1e-3)) + math.log(1e-3))
    return {
        "x": nrm(ks[0], (BATCH, SEQ, D_MODEL), 1.0),
        "norm_mix": 1.0 + nrm(ks[1], (DEPTH, D_MODEL), 0.02),
        "w_in": nrm(ks[2], (DEPTH, D_MODEL, IN_DIM), D_MODEL ** -0.5),
        "conv_w": nrm(ks[3], (DEPTH, CONV_K, CONV_CH), CONV_K ** -0.5),
        "conv_b": nrm(ks[4], (DEPTH, CONV_CH), 0.02),
        "dt_bias": dt0 + jnp.log(-jnp.expm1(-dt0)),
        "a_log": jnp.log(jax.random.uniform(ks[6], (DEPTH, SSD_HEADS), f32, 1.0, 16.0)),
        "d_skip": 1.0 + nrm(ks[7], (DEPTH, SSD_HEADS), 0.02),
        "ssd_norm": 1.0 + nrm(ks[8], (DEPTH, D_SSD), 0.02),
        "attn_sinks": nrm(ks[9], (DEPTH, ATTN_Q_HEADS), 0.5),
        "attn_norm": 1.0 + nrm(ks[10], (DEPTH, D_ATTN), 0.02),
        "w_out": nrm(ks[11], (DEPTH, D_MIX, D_MODEL), D_MIX ** -0.5),
        "norm_ffn": 1.0 + nrm(ks[12], (DEPTH, D_MODEL), 0.02),
        "w_router_group": nrm(ks[13], (DEPTH, D_MODEL, MOE_GROUPS), D_MODEL ** -0.5),
        "w_router_expert": nrm(ks[14], (DEPTH, D_MODEL, N_EXPERTS), D_MODEL ** -0.5),
        "w_gate": nrm(ks[15], (DEPTH, N_EXPERTS, D_MODEL, D_FF_EXPERT), D_MODEL ** -0.5),
        "w_up": nrm(ks[16], (DEPTH, N_EXPERTS, D_MODEL, D_FF_EXPERT), D_MODEL ** -0.5),
        "w_down": nrm(ks[17], (DEPTH, N_EXPERTS, D_FF_EXPERT, D_MODEL), D_FF_EXPERT ** -0.5),
        "norm_final": 1.0 + nrm(ks[18], (D_MODEL,), 0.02),
    }


def reference(x, norm_mix, w_in, conv_w, conv_b, dt_bias, a_log, d_skip, ssd_norm,
              attn_sinks, attn_norm, w_out, norm_ffn, w_router_group, w_router_expert,
              w_gate, w_up, w_down, norm_final):
    h = x
    for l in range(DEPTH):
        h = h + hybrid_mixer(rmsnorm(h, norm_mix[l]), w_in[l], conv_w[l], conv_b[l],
                             dt_bias[l], a_log[l], d_skip[l], ssd_norm[l],
                             attn_sinks[l], attn_norm[l], w_out[l])
        h = h + hierarchical_moe(rmsnorm(h, norm_ffn[l]), w_router_group[l],
                                 w_router_expert[l], w_gate[l], w_up[l], w_down[l])
    return rmsnorm(h, norm_final)
```

```python
import functools

import jax
import jax.numpy as jnp
from jax import lax
from jax.experimental import pallas as pl
from jax.experimental.pallas import tpu as pltpu

F32 = jnp.float32
BF16 = jnp.bfloat16
I32 = jnp.int32

EPS = 1e-6
SSD_HEAD_DIM = 64
SSD_GROUPS = 2
SSD_STATE = 128
CONV_K = 4
ATTN_HEAD_DIM = 64
ATTN_KV_HEADS = 2
WINDOW = 128
MOE_GROUPS = 8
EXPERTS_PER_GROUP = 8
TOP_K = 2

LANES = 128
SUBLANES = 8
SSD_CHUNK = 128
EXPERT_BLOCK = 128
TOKEN_TILE = 512
ROW_TILE = 256
VMEM_LIMIT = 56 * 1024 * 1024


def _dot(a, b):
    return jnp.dot(a, b, preferred_element_type=F32)


def _dot_nt(a, b):
    return lax.dot_general(a, b, (((1,), (1,)), ((), ())), preferred_element_type=F32)


def _dot_tn(a, b):
    return lax.dot_general(a, b, (((0,), (0,)), ((), ())), preferred_element_type=F32)


def _silu(x):
    return x * jax.nn.sigmoid(x)


def _rms(x, gain):
    return x * lax.rsqrt(jnp.mean(x * x, axis=-1, keepdims=True) + EPS) * gain


def _inproj_kernel(x_ref, g_ref, wz_ref, wxbc_ref, wdt_ref, wq_ref, wkv_ref, dtb_ref,
                   z_ref, xbc_ref, dt_ref, q_ref, kv_ref, *, q_scale):
    h = _rms(x_ref[...], g_ref[...]).astype(BF16)
    z_ref[...] = _dot(h, wz_ref[...])
    xbc_ref[...] = _dot(h, wxbc_ref[...])
    dt_ref[...] = jax.nn.softplus(_dot(h, wdt_ref[...]) + dtb_ref[...])
    q_ref[...] = (_dot(h, wq_ref[...]) * q_scale).astype(BF16)
    kv_ref[...] = _dot(h, wkv_ref[...]).astype(BF16)


def _inproj(x2, g, wz, wxbc, wdt, wq, wkv, dtb, q_scale):
    n, d = x2.shape
    tm = TOKEN_TILE
    row = lambda w: pl.BlockSpec((tm, w), lambda i: (i, 0))
    full = lambda a: pl.BlockSpec(a.shape, lambda i: (0, 0))
    outs = [(wz.shape[1], F32), (wxbc.shape[1], F32), (wdt.shape[1], F32),
            (wq.shape[1], BF16), (wkv.shape[1], BF16)]
    return pl.pallas_call(
        functools.partial(_inproj_kernel, q_scale=q_scale),
        grid=(n // tm,),
        in_specs=[row(d), full(g), full(wz), full(wxbc), full(wdt), full(wq), full(wkv), full(dtb)],
        out_specs=[row(w) for w, _ in outs],
        out_shape=[jax.ShapeDtypeStruct((n, w), t) for w, t in outs],
        compiler_params=pltpu.CompilerParams(dimension_semantics=("parallel",),
                                             vmem_limit_bytes=VMEM_LIMIT),
        name="inproj",
    )(x2, g, wz, wxbc, wdt, wq, wkv, dtb)


def _expand_heads(v, n_heads, head_dim):
    r = v.shape[0]
    lane = lax.broadcasted_iota(I32, (r, LANES), 1)
    per_tile = LANES // head_dim
    tiles = []
    for t in range(n_heads // per_tile):
        acc = jnp.broadcast_to(v[:, t * per_tile:t * per_tile + 1], (r, LANES))
        for j in range(1, per_tile):
            h = t * per_tile + j
            acc = jnp.where(lane < j * head_dim, acc, jnp.broadcast_to(v[:, h:h + 1], (r, LANES)))
        tiles.append(acc)
    return jnp.concatenate(tiles, axis=1)


def _ssd_kernel(xbc_ref, z_ref, dt_ref, cw_ref, cb_ref, alog_ref, dskip_ref, gn_ref,
                y_ref, ext_ref, st_ref, *, d_ssd, n_heads):
    L = SSD_CHUNK
    P = SSD_HEAD_DIM
    NS = SSD_STATE
    G = SSD_GROUPS
    gw = d_ssd // G
    halo = SUBLANES

    @pl.when(pl.program_id(1) == 0)
    def _():
        ext_ref[0:halo, :] = jnp.zeros((halo, ext_ref.shape[1]), F32)
        st_ref[...] = jnp.zeros(st_ref.shape, F32)

    u = xbc_ref[0]
    ext_ref[halo:halo + L, :] = u
    acc = cb_ref[...] + cw_ref[CONV_K - 1:CONV_K, :] * u
    for j in range(CONV_K - 1):
        off = halo - (CONV_K - 1) + j
        acc = acc + cw_ref[j:j + 1, :] * ext_ref[off:off + L, :]
    ext_ref[0:halo, :] = u[L - halo:L, :]
    xa = _silu(acc)
    xs = xa[:, :d_ssd]

    dt = dt_ref[0]
    a = dt * (-jnp.exp(alog_ref[...]))
    r_i = lax.broadcasted_iota(I32, (L, L), 0)
    c_i = lax.broadcasted_iota(I32, (L, L), 1)
    causal = r_i >= c_i
    tri = jnp.where(causal, 1.0, 0.0).astype(BF16)
    a_hi = a.astype(BF16)
    a_r1 = a - a_hi.astype(F32)
    a_mid = a_r1.astype(BF16)
    a_lo = (a_r1 - a_mid.astype(F32)).astype(BF16)
    a_cs = _dot(tri, a_hi) + _dot(tri, a_mid) + _dot(tri, a_lo)
    a_cs_t = a_cs.T
    a_last = a_cs[L - 1:L, :]
    decay_end = jnp.exp(a_last - a_cs)
    decay_in = jnp.exp(a_cs)
    chunk_decay = jnp.exp(a_last)

    xdt = xs * _expand_heads(dt, n_heads, P)
    xdt_b = xdt.astype(BF16)
    xdt_end_b = (xdt * _expand_heads(decay_end, n_heads, P)).astype(BF16)
    decay_in_e = _expand_heads(decay_in, n_heads, P)
    chunk_decay_e = _expand_heads(chunk_decay, n_heads, P)

    lane = lax.broadcasted_iota(I32, (L, LANES), 1)
    heads_per_tile = LANES // P
    y_tiles = []
    y_off = []
    for g in range(G):
        b_g = xa[:, d_ssd + g * NS:d_ssd + (g + 1) * NS].astype(BF16)
        c_g = xa[:, d_ssd + G * NS + g * NS:d_ssd + G * NS + (g + 1) * NS].astype(BF16)
        cb = _dot_nt(c_g, b_g)
        st = st_ref[g]
        y_off.append(_dot(c_g, st.astype(BF16)))
        st_ref[g] = st * chunk_decay_e[:, g * gw:(g + 1) * gw] + _dot_tn(b_g, xdt_end_b[:, g * gw:(g + 1) * gw])
        for t in range(gw // LANES):
            tile_idx = g * (gw // LANES) + t
            x_tile = xdt_b[:, tile_idx * LANES:(tile_idx + 1) * LANES]
            ms = []
            ws = []
            for j in range(heads_per_tile):
                h = tile_idx * heads_per_tile + j
                seg = a_cs[:, h:h + 1] - a_cs_t[h:h + 1, :]
                dec = jnp.exp(jnp.where(causal, seg, -jnp.inf))
                ms.append((cb * dec).astype(BF16))
                in_head = (lane >= j * P) & (lane < (j + 1) * P)
                ws.append(jnp.where(in_head, x_tile, jnp.zeros_like(x_tile)))
            y_tiles.append(_dot(jnp.concatenate(ms, axis=1), jnp.concatenate(ws, axis=0)))
    y = (jnp.concatenate(y_tiles, axis=1) + jnp.concatenate(y_off, axis=1) * decay_in_e
         + xs * dskip_ref[...])

    gated = y * _silu(z_ref[0])
    outs = []
    for g in range(G):
        outs.append(_rms(gated[:, g * gw:(g + 1) * gw], gn_ref[:, g * gw:(g + 1) * gw]))
    y_ref[0] = jnp.concatenate(outs, axis=1).astype(y_ref.dtype)


def _ssd(xbc, z, dt, conv_w, conv_b, alog, dskip_e, gn, d_ssd, n_heads):
    b, s, ch = xbc.shape
    L = SSD_CHUNK
    tile = lambda w: pl.BlockSpec((1, L, w), lambda i, c: (i, c, 0))
    full = lambda a: pl.BlockSpec(a.shape, lambda i, c: (0, 0))
    return pl.pallas_call(
        functools.partial(_ssd_kernel, d_ssd=d_ssd, n_heads=n_heads),
        grid=(b, s // L),
        in_specs=[tile(ch), tile(d_ssd), tile(LANES), full(conv_w), full(conv_b), full(alog),
                  full(dskip_e), full(gn)],
        out_specs=tile(d_ssd),
        out_shape=jax.ShapeDtypeStruct((b, s, d_ssd), BF16),
        scratch_shapes=[pltpu.VMEM((SUBLANES + L, ch), F32),
                        pltpu.VMEM((SSD_GROUPS, SSD_STATE, d_ssd // SSD_GROUPS), F32)],
        compiler_params=pltpu.CompilerParams(dimension_semantics=("parallel", "arbitrary"),
                                             vmem_limit_bytes=VMEM_LIMIT),
        name="ssd",
    )(xbc, z, dt, conv_w, conv_b, alog, dskip_e, gn)


def _attn_kernel(sink_ref, q_ref, kvc_ref, kvp_ref, an_ref, o_ref, att_ref, *, n_q_heads):
    W = WINDOW
    P = ATTN_HEAD_DIM
    n = pl.program_id(1)
    kv_w = ATTN_KV_HEADS * P
    kv = jnp.concatenate([kvp_ref[0], kvc_ref[0]], axis=0)
    kk = kv[:, :kv_w]
    vv = kv[:, kv_w:]
    swap = lambda x: jnp.concatenate([x[:, P:], x[:, :P]], axis=1)
    kk_s = swap(kk)
    vv_s = swap(vv)
    lane = lax.broadcasted_iota(I32, (2 * W, LANES), 1)
    low = lane < P
    zero = jnp.zeros((2 * W, LANES), kv.dtype)

    def blockdiag(x, x_s, g):
        if g == 0:
            top, bot = jnp.where(low, x, zero), jnp.where(low, zero, x_s)
        else:
            top, bot = jnp.where(low, x_s, zero), jnp.where(low, zero, x)
        return jnp.concatenate([top, bot], axis=0)

    qi = lax.broadcasted_iota(I32, (W, 2 * W), 0)
    kj = lax.broadcasted_iota(I32, (W, 2 * W), 1)
    rel = qi + W - kj
    valid = (rel >= 0) & (rel < W) & ((n > 0) | (kj >= W))
    valid2 = jnp.concatenate([valid, valid], axis=1)
    lane_w = lax.broadcasted_iota(I32, (W, LANES), 1)
    pairs_per_kv = n_q_heads // ATTN_KV_HEADS // 2
    for g in range(ATTN_KV_HEADS):
        k_bd = blockdiag(kk, kk_s, g)
        v_bd = blockdiag(vv, vv_s, g)
        for p in range(pairs_per_kv):
            hp = g * pairs_per_kv + p
            s = _dot_nt(q_ref[0, :, hp * LANES:(hp + 1) * LANES], k_bd)
            s = jnp.where(valid2, s, -jnp.inf)
            probs = []
            denoms = []
            for j in range(2):
                sink = sink_ref[2 * hp + j]
                sj = s[:, j * 2 * W:(j + 1) * 2 * W]
                m = jnp.maximum(jnp.max(sj, axis=-1, keepdims=True), sink)
                pj = jnp.exp(sj - m)
                probs.append(pj.astype(BF16))
                denoms.append(jnp.sum(pj, axis=-1, keepdims=True) + jnp.exp(sink - m))
            o = _dot(jnp.concatenate(probs, axis=1), v_bd)
            den = jnp.where(lane_w < P, jnp.broadcast_to(denoms[0], (W, LANES)),
                            jnp.broadcast_to(denoms[1], (W, LANES)))
            att_ref[:, hp * LANES:(hp + 1) * LANES] = o / den
    o_ref[0] = _rms(att_ref[...], an_ref[...]).astype(o_ref.dtype)


def _attn(sinks, q, kv, an, n_q_heads):
    b, s, dq = q.shape
    W = WINDOW
    grid_spec = pltpu.PrefetchScalarGridSpec(
        num_scalar_prefetch=1,
        grid=(b, s // W),
        in_specs=[pl.BlockSpec((1, W, dq), lambda i, n, sk: (i, n, 0)),
                  pl.BlockSpec((1, W, kv.shape[2]), lambda i, n, sk: (i, n, 0)),
                  pl.BlockSpec((1, W, kv.shape[2]), lambda i, n, sk: (i, jnp.maximum(n - 1, 0), 0)),
                  pl.BlockSpec(an.shape, lambda i, n, sk: (0, 0))],
        out_specs=pl.BlockSpec((1, W, dq), lambda i, n, sk: (i, n, 0)),
        scratch_shapes=[pltpu.VMEM((W, dq), F32)],
    )
    return pl.pallas_call(
        functools.partial(_attn_kernel, n_q_heads=n_q_heads),
        grid_spec=grid_spec,
        out_shape=jax.ShapeDtypeStruct((b, s, dq), BF16),
        compiler_params=pltpu.CompilerParams(dimension_semantics=("parallel", "arbitrary"),
                                             vmem_limit_bytes=VMEM_LIMIT),
        name="attn",
    )(sinks, q, kv, kv, an)


def _outproj_kernel(x_ref, ys_ref, ya_ref, wos_ref, woa_ref, nf_ref, wr_ref, tri_ref,
                    h1_ref, h2_ref, ri_ref, gt_ref, cnt_ref, base_ref):
    tm = x_ref.shape[0]
    G = MOE_GROUPS
    E = EXPERTS_PER_GROUP
    NE = G * E

    @pl.when(pl.program_id(0) == 0)
    def _():
        base_ref[...] = jnp.zeros(base_ref.shape, F32)

    h1 = x_ref[...] + _dot(ys_ref[...], wos_ref[...]) + _dot(ya_ref[...], woa_ref[...])
    h1_ref[...] = h1
    h2 = _rms(h1, nf_ref[...])
    h2_ref[...] = h2
    logit_t = _dot_nt(wr_ref[...], h2.astype(BF16))
    gl = logit_t[0:G]
    sub = lax.broadcasted_iota(I32, (G, tm), 0)
    gmax = jnp.max(gl, axis=0, keepdims=True)
    g_sel = jnp.min(jnp.where(gl == gmax, sub, G), axis=0, keepdims=True)
    p_group = 1.0 / jnp.sum(jnp.exp(gl - gmax), axis=0, keepdims=True)
    in_group = jnp.zeros((E, tm), F32)
    for g in range(G):
        in_group = jnp.where(g_sel == g, logit_t[G + g * E:G + (g + 1) * E], in_group)
    t1 = jnp.max(in_group, axis=0, keepdims=True)
    i1 = jnp.min(jnp.where(in_group == t1, sub, E), axis=0, keepdims=True)
    rest = jnp.where(sub == i1, -jnp.inf, in_group)
    t2 = jnp.max(rest, axis=0, keepdims=True)
    i2 = jnp.min(jnp.where(rest == t2, sub, E), axis=0, keepdims=True)
    e2 = jnp.exp(t2 - t1)
    gate1 = p_group / (1.0 + e2)
    gate2 = p_group * e2 / (1.0 + e2)
    eid1 = g_sel * E + i1
    eid2 = g_sel * E + i2

    e_iota = lax.broadcasted_iota(I32, (NE, tm), 0)
    hot1 = e_iota == eid1
    hot2 = e_iota == eid2
    two_hot = jnp.where(hot1 | hot2, 1.0, 0.0).astype(BF16)
    pref = _dot(two_hot, tri_ref[...])
    base = base_ref[...]
    before = pref[:, :tm] + jnp.concatenate([base] * (tm // LANES), axis=1)
    rank1 = jnp.sum(jnp.where(hot1, before, 0.0), axis=0, keepdims=True)
    rank2 = jnp.sum(jnp.where(hot2, before, 0.0), axis=0, keepdims=True)
    base = base + pref[:, tm:]
    base_ref[...] = base
    cnt_ref[...] = base

    ri_ref[...] = jnp.zeros(ri_ref.shape, I32)
    ri_ref[0:1, :] = eid1
    ri_ref[1:2, :] = eid2
    ri_ref[2:3, :] = rank1.astype(I32)
    ri_ref[3:4, :] = rank2.astype(I32)
    gates_t = jnp.concatenate([gate1, gate2, jnp.zeros((LANES - 2, tm), F32)], axis=0)
    for c in range(tm // LANES):
        gt_ref[c * LANES:(c + 1) * LANES, :] = gates_t[:, c * LANES:(c + 1) * LANES].T


def _outproj(x2, ys, ya, wos, woa, nf, wr_t, tri):
    n, d = x2.shape
    tm = TOKEN_TILE
    ne = MOE_GROUPS * EXPERTS_PER_GROUP
    row = lambda w: pl.BlockSpec((tm, w), lambda i: (i, 0))
    full = lambda a: pl.BlockSpec(a.shape, lambda i: (0, 0))
    return pl.pallas_call(
        _outproj_kernel,
        grid=(n // tm,),
        in_specs=[row(d), row(ys.shape[1]), row(ya.shape[1]), full(wos), full(woa), full(nf),
                  full(wr_t), full(tri)],
        out_specs=[row(d), row(d), pl.BlockSpec((SUBLANES, tm), lambda i: (0, i)), row(LANES),
                   pl.BlockSpec((ne, LANES), lambda i: (0, 0))],
        out_shape=[jax.ShapeDtypeStruct((n, d), F32), jax.ShapeDtypeStruct((n, d), F32),
                   jax.ShapeDtypeStruct((SUBLANES, n), I32), jax.ShapeDtypeStruct((n, LANES), F32),
                   jax.ShapeDtypeStruct((ne, LANES), F32)],
        scratch_shapes=[pltpu.VMEM((ne, LANES), F32)],
        compiler_params=pltpu.CompilerParams(dimension_semantics=("arbitrary",),
                                             vmem_limit_bytes=VMEM_LIMIT),
        name="outproj_router",
    )(x2, ys, ya, wos, woa, nf, wr_t, tri)


def _dest_kernel(ri_ref, ps_ref, d_ref):
    ne = ps_ref.shape[0]
    tm = ri_ref.shape[1]
    e_iota = lax.broadcasted_iota(I32, (ne, tm), 0)
    ps = jnp.concatenate([ps_ref[...]] * (tm // LANES), axis=1)
    d_ref[...] = jnp.zeros(d_ref.shape, I32)
    for k in range(TOP_K):
        start = jnp.sum(jnp.where(e_iota == ri_ref[k:k + 1, :], ps, 0), axis=0, keepdims=True)
        d_ref[k:k + 1, :] = start + ri_ref[TOP_K + k:TOP_K + k + 1, :]


def _dest(route_i, pstart_rep):
    n = route_i.shape[1]
    tm = min(2048, n)
    return pl.pallas_call(
        _dest_kernel,
        grid=(n // tm,),
        in_specs=[pl.BlockSpec((SUBLANES, tm), lambda i: (0, i)),
                  pl.BlockSpec(pstart_rep.shape, lambda i: (0, 0))],
        out_specs=pl.BlockSpec((SUBLANES, tm), lambda i: (0, i)),
        out_shape=jax.ShapeDtypeStruct((SUBLANES, n), I32),
        compiler_params=pltpu.CompilerParams(dimension_semantics=("parallel",)),
        name="dest_rows",
    )(route_i, pstart_rep)


def _dispatch_kernel(dest_ref, h_ref, buf_in_ref, buf_ref, sem, *, n_tokens):
    del buf_in_ref
    tm = h_ref.shape[0]
    base = pl.program_id(0) * tm

    def issue(t, carry):
        for k in range(TOP_K):
            d = dest_ref[k * n_tokens + base + t]
            pltpu.make_async_copy(h_ref.at[pl.ds(t, 1)], buf_ref.at[pl.ds(d, 1)], sem).start()
        return carry

    lax.fori_loop(0, tm, issue, 0)

    def drain(t, carry):
        for k in range(TOP_K):
            pltpu.make_async_copy(h_ref.at[pl.ds(0, 1)], buf_ref.at[pl.ds(0, 1)], sem).wait()
        return carry

    lax.fori_loop(0, tm, drain, 0)


def _dispatch(dest_flat, h2, buf0):
    n, d = h2.shape
    tm = ROW_TILE
    grid_spec = pltpu.PrefetchScalarGridSpec(
        num_scalar_prefetch=1,
        grid=(n // tm,),
        in_specs=[pl.BlockSpec((tm, d), lambda i, ds: (i, 0)),
                  pl.BlockSpec(memory_space=pl.ANY)],
        out_specs=pl.BlockSpec(memory_space=pl.ANY),
        scratch_shapes=[pltpu.SemaphoreType.DMA(())],
    )
    return pl.pallas_call(
        functools.partial(_dispatch_kernel, n_tokens=n),
        grid_spec=grid_spec,
        out_shape=jax.ShapeDtypeStruct(buf0.shape, buf0.dtype),
        input_output_aliases={2: 0},
        compiler_params=pltpu.CompilerParams(dimension_semantics=("arbitrary",)),
        name="dispatch",
    )(dest_flat, h2, buf0)


def _combine_kernel(dest_ref, ybuf_ref, h1_ref, gt_ref, nfin_ref, o_ref, rows_ref, sem, *, n_tokens):
    tm = h1_ref.shape[0]
    base = pl.program_id(0) * tm

    def issue(t, carry):
        for k in range(TOP_K):
            d = dest_ref[k * n_tokens + base + t]
            pltpu.make_async_copy(ybuf_ref.at[pl.ds(d, 1)], rows_ref.at[k, pl.ds(t, 1)], sem).start()
        return carry

    lax.fori_loop(0, tm, issue, 0)

    def drain(t, carry):
        for k in range(TOP_K):
            pltpu.make_async_copy(ybuf_ref.at[pl.ds(0, 1)], rows_ref.at[k, pl.ds(0, 1)], sem).wait()
        return carry

    lax.fori_loop(0, tm, drain, 0)

    y = jnp.zeros(h1_ref.shape, F32)
    for k in range(TOP_K):
        y = y + rows_ref[k] * gt_ref[:, k:k + 1]
    o_ref[...] = _rms(h1_ref[...] + y, nfin_ref[...])


def _combine(dest_flat, ybuf, h1, gates_tm, nfin):
    n, d = h1.shape
    tm = ROW_TILE
    grid_spec = pltpu.PrefetchScalarGridSpec(
        num_scalar_prefetch=1,
        grid=(n // tm,),
        in_specs=[pl.BlockSpec(memory_space=pl.ANY),
                  pl.BlockSpec((tm, d), lambda i, ds: (i, 0)),
                  pl.BlockSpec((tm, LANES), lambda i, ds: (i, 0)),
                  pl.BlockSpec(nfin.shape, lambda i, ds: (0, 0))],
        out_specs=pl.BlockSpec((tm, d), lambda i, ds: (i, 0)),
        scratch_shapes=[pltpu.VMEM((TOP_K, tm, d), F32), pltpu.SemaphoreType.DMA(())],
    )
    return pl.pallas_call(
        functools.partial(_combine_kernel, n_tokens=n),
        grid_spec=grid_spec,
        out_shape=jax.ShapeDtypeStruct((n, d), F32),
        compiler_params=pltpu.CompilerParams(dimension_semantics=("arbitrary",),
                                             vmem_limit_bytes=VMEM_LIMIT),
        name="combine",
    )(dest_flat, ybuf, h1, gates_tm, nfin)


def _expert_kernel(be_ref, nb_ref, x_ref, wg_ref, wu_ref, wd_ref, y_ref):
    del be_ref
    used = pl.program_id(0) < nb_ref[0]

    @pl.when(used)
    def _():
        xb = x_ref[...].astype(BF16)
        gate = _dot(xb, wg_ref[0].astype(BF16))
        up = _dot(xb, wu_ref[0].astype(BF16))
        hid = (_silu(gate) * up).astype(BF16)
        y_ref[...] = _dot(hid, wd_ref[0].astype(BF16))

    @pl.when(jnp.logical_not(used))
    def _():
        y_ref[...] = jnp.zeros(y_ref.shape, F32)


def _experts(block_expert, n_used, buf, w_gate, w_up, w_down):
    rows, d = buf.shape
    blk = EXPERT_BLOCK
    f = w_gate.shape[2]
    grid_spec = pltpu.PrefetchScalarGridSpec(
        num_scalar_prefetch=2,
        grid=(rows // blk,),
        in_specs=[pl.BlockSpec((blk, d), lambda i, be, nb: (jnp.minimum(i, nb[0] - 1), 0)),
                  pl.BlockSpec((1, d, f), lambda i, be, nb: (be[i], 0, 0)),
                  pl.BlockSpec((1, d, f), lambda i, be, nb: (be[i], 0, 0)),
                  pl.BlockSpec((1, f, d), lambda i, be, nb: (be[i], 0, 0))],
        out_specs=pl.BlockSpec((blk, d), lambda i, be, nb: (i, 0)),
    )
    return pl.pallas_call(
        _expert_kernel,
        grid_spec=grid_spec,
        out_shape=jax.ShapeDtypeStruct((rows, d), F32),
        compiler_params=pltpu.CompilerParams(dimension_semantics=("arbitrary",),
                                             vmem_limit_bytes=VMEM_LIMIT),
        name="experts",
    )(block_expert, n_used, buf, w_gate, w_up, w_down)


def _layer(h, norm_mix, w_in, conv_w, conv_b, dt_bias, a_log, d_skip, ssd_norm, attn_sinks,
           attn_norm, w_out, norm_ffn, w_rg, w_re, w_gate, w_up, w_down, final_gain):
    b, s, d = h.shape
    n = b * s
    n_heads = dt_bias.shape[0]
    d_ssd = n_heads * SSD_HEAD_DIM
    conv_ch = conv_w.shape[1]
    n_q_heads = attn_sinks.shape[0]
    d_attn = n_q_heads * ATTN_HEAD_DIM
    ne = MOE_GROUPS * EXPERTS_PER_GROUP
    row = lambda v: v.reshape(1, -1)
    pad_lanes = lambda v: jnp.pad(v, ((0, 0), (0, LANES - v.shape[1])))

    o1, o2, o3, o4 = d_ssd, d_ssd + conv_ch, d_ssd + conv_ch + n_heads, d_ssd + conv_ch + n_heads + d_attn
    wb = w_in.astype(BF16)
    x2 = h.reshape(n, d)
    z, xbc, dt, q, kv = _inproj(
        x2, row(norm_mix), wb[:, :o1], wb[:, o1:o2], pad_lanes(wb[:, o2:o3]), wb[:, o3:o4], wb[:, o4:],
        pad_lanes(row(dt_bias)), ATTN_HEAD_DIM ** -0.5)

    y_ssd = _ssd(xbc.reshape(b, s, conv_ch), z.reshape(b, s, d_ssd), dt.reshape(b, s, LANES),
                 conv_w, row(conv_b), pad_lanes(row(a_log)), row(jnp.repeat(d_skip, SSD_HEAD_DIM)),
                 row(ssd_norm), d_ssd, n_heads)
    y_att = _attn(attn_sinks, q.reshape(b, s, d_attn), kv.reshape(b, s, kv.shape[1]), row(attn_norm),
                  n_q_heads)

    wo = w_out.astype(BF16)
    wr_t = jnp.pad(jnp.concatenate([w_rg, w_re], axis=1).T, ((0, LANES - MOE_GROUPS - ne), (0, 0))).astype(BF16)
    tm = TOKEN_TILE
    tri = jnp.concatenate([jnp.triu(jnp.ones((tm, tm), BF16), 1), jnp.ones((tm, LANES), BF16)], axis=1)
    h1, h2, route_i, gates_tm, counts = _outproj(
        x2, y_ssd.reshape(n, d_ssd), y_att.reshape(n, d_attn), wo[:d_ssd], wo[d_ssd:], row(norm_ffn),
        wr_t, tri)

    blk = EXPERT_BLOCK
    cnt = counts[:, 0].astype(I32)
    padded = (cnt + blk - 1) // blk * blk
    pend = jnp.cumsum(padded)
    pstart = pend - padded
    n_blocks = -(-(n * TOP_K + ne * (blk - 1)) // blk)
    n_used = (pend[-1] // blk).astype(I32)
    blk_start = jnp.arange(n_blocks, dtype=I32) * blk
    blk_e = jnp.minimum(jnp.sum((pend[None, :] <= blk_start[:, None]).astype(I32), axis=1), ne - 1)
    last_e = jnp.max(jnp.where(cnt > 0, jnp.arange(ne, dtype=I32), 0))
    blk_e = jnp.where(jnp.arange(n_blocks) < n_used, blk_e, last_e).astype(I32)

    dest = _dest(route_i, jnp.broadcast_to(pstart[:, None], (ne, LANES)).astype(I32))
    dest_flat = dest[:TOP_K].reshape(-1)
    buf = _dispatch(dest_flat, h2, jnp.zeros((n_blocks * blk, d), F32))
    ybuf = _experts(blk_e, n_used.reshape(1), buf, w_gate, w_up, w_down)
    out = _combine(dest_flat, ybuf, h1, gates_tm, row(final_gain))
    return out.reshape(b, s, d)


def kernel(x, norm_mix, w_in, conv_w, conv_b, dt_bias, a_log, d_skip, ssd_norm, attn_sinks, attn_norm,
           w_out, norm_ffn, w_router_group, w_router_expert, w_gate, w_up, w_down, norm_final):
    depth = w_in.shape[0]
    assert depth == 1, "the fused final rmsnorm assumes a single layer"
    return _layer(x, norm_mix[0], w_in[0], conv_w[0], conv_b[0], dt_bias[0], a_log[0], d_skip[0],
                  ssd_norm[0], attn_sinks[0], attn_norm[0], w_out[0], norm_ffn[0], w_router_group[0],
                  w_router_expert[0], w_gate[0], w_up[0], w_down[0], norm_final)
```

```python
import functools

import jax
import jax.numpy as jnp
from jax import lax
from jax.experimental import pallas as pl
from jax.experimental.pallas import tpu as pltpu

F32 = jnp.float32
BF16 = jnp.bfloat16
I32 = jnp.int32

EPS = 1e-6
SSD_HEAD_DIM = 64
SSD_GROUPS = 2
SSD_STATE = 128
CONV_K = 4
ATTN_HEAD_DIM = 64
ATTN_KV_HEADS = 2
WINDOW = 128
MOE_GROUPS = 8
EXPERTS_PER_GROUP = 8
TOP_K = 2

LANES = 128
SUBLANES = 8
SSD_CHUNK = 128
EXPERT_BLOCK = 256
DMA_UNROLL = 8
TOKEN_TILE = 512
ROW_TILE = 256
VMEM_LIMIT = 56 * 1024 * 1024


def _dot(a, b):
    return jnp.dot(a, b, preferred_element_type=F32)


def _dot_nt(a, b):
    return lax.dot_general(a, b, (((1,), (1,)), ((), ())), preferred_element_type=F32)


def _dot_tn(a, b):
    return lax.dot_general(a, b, (((0,), (0,)), ((), ())), preferred_element_type=F32)


def _silu(x):
    return x * jax.nn.sigmoid(x)


def _rms(x, gain):
    return x * lax.rsqrt(jnp.mean(x * x, axis=-1, keepdims=True) + EPS) * gain


def _inproj_kernel(x_ref, g_ref, wz_ref, wxbc_ref, wdt_ref, wq_ref, wkv_ref, dtb_ref,
                   z_ref, xbc_ref, dt_ref, q_ref, kv_ref, *, q_scale):
    h = _rms(x_ref[...], g_ref[...]).astype(BF16)
    z_ref[...] = _dot(h, wz_ref[...])
    xbc_ref[...] = _dot(h, wxbc_ref[...])
    dt_ref[...] = jax.nn.softplus(_dot(h, wdt_ref[...]) + dtb_ref[...])
    q_ref[...] = (_dot(h, wq_ref[...]) * q_scale).astype(BF16)
    kv_ref[...] = _dot(h, wkv_ref[...]).astype(BF16)


def _inproj(x2, g, wz, wxbc, wdt, wq, wkv, dtb, q_scale):
    n, d = x2.shape
    tm = TOKEN_TILE
    row = lambda w: pl.BlockSpec((tm, w), lambda i: (i, 0))
    full = lambda a: pl.BlockSpec(a.shape, lambda i: (0, 0))
    outs = [(wz.shape[1], F32), (wxbc.shape[1], F32), (wdt.shape[1], F32),
            (wq.shape[1], BF16), (wkv.shape[1], BF16)]
    return pl.pallas_call(
        functools.partial(_inproj_kernel, q_scale=q_scale),
        grid=(n // tm,),
        in_specs=[row(d), full(g), full(wz), full(wxbc), full(wdt), full(wq), full(wkv), full(dtb)],
        out_specs=[row(w) for w, _ in outs],
        out_shape=[jax.ShapeDtypeStruct((n, w), t) for w, t in outs],
        compiler_params=pltpu.CompilerParams(dimension_semantics=("parallel",),
                                             vmem_limit_bytes=VMEM_LIMIT),
        name="inproj",
    )(x2, g, wz, wxbc, wdt, wq, wkv, dtb)


def _expand_heads(v, n_heads, head_dim):
    r = v.shape[0]
    lane = lax.broadcasted_iota(I32, (r, LANES), 1)
    per_tile = LANES // head_dim
    tiles = []
    for t in range(n_heads // per_tile):
        acc = jnp.broadcast_to(v[:, t * per_tile:t * per_tile + 1], (r, LANES))
        for j in range(1, per_tile):
            h = t * per_tile + j
            acc = jnp.where(lane < j * head_dim, acc, jnp.broadcast_to(v[:, h:h + 1], (r, LANES)))
        tiles.append(acc)
    return jnp.concatenate(tiles, axis=1)


def _ssd_kernel(xbc_ref, z_ref, dt_ref, cw_ref, cb_ref, alog_ref, dskip_ref, gn_ref,
                y_ref, ext_ref, st_ref, *, d_ssd, n_heads):
    L = SSD_CHUNK
    P = SSD_HEAD_DIM
    NS = SSD_STATE
    G = SSD_GROUPS
    gw = d_ssd // G
    halo = SUBLANES

    @pl.when(pl.program_id(1) == 0)
    def _():
        ext_ref[0:halo, :] = jnp.zeros((halo, ext_ref.shape[1]), F32)
        st_ref[...] = jnp.zeros(st_ref.shape, F32)

    u = xbc_ref[0]
    ext_ref[halo:halo + L, :] = u
    acc = cb_ref[...] + cw_ref[CONV_K - 1:CONV_K, :] * u
    for j in range(CONV_K - 1):
        off = halo - (CONV_K - 1) + j
        acc = acc + cw_ref[j:j + 1, :] * ext_ref[off:off + L, :]
    ext_ref[0:halo, :] = u[L - halo:L, :]
    xa = _silu(acc)
    xs = xa[:, :d_ssd]

    dt = dt_ref[0]
    a = dt * (-jnp.exp(alog_ref[...]))
    r_i = lax.broadcasted_iota(I32, (L, L), 0)
    c_i = lax.broadcasted_iota(I32, (L, L), 1)
    causal = r_i >= c_i
    tri = jnp.where(causal, 1.0, 0.0).astype(BF16)
    a_hi = a.astype(BF16)
    a_r1 = a - a_hi.astype(F32)
    a_mid = a_r1.astype(BF16)
    a_lo = (a_r1 - a_mid.astype(F32)).astype(BF16)
    a_cs = _dot(tri, a_hi) + _dot(tri, a_mid) + _dot(tri, a_lo)
    a_cs_t = a_cs.T
    a_last = a_cs[L - 1:L, :]
    decay_end = jnp.exp(a_last - a_cs)
    decay_in = jnp.exp(a_cs)
    chunk_decay = jnp.exp(a_last)

    xdt = xs * _expand_heads(dt, n_heads, P)
    xdt_b = xdt.astype(BF16)
    xdt_end_b = (xdt * _expand_heads(decay_end, n_heads, P)).astype(BF16)
    decay_in_e = _expand_heads(decay_in, n_heads, P)
    chunk_decay_e = _expand_heads(chunk_decay, n_heads, P)

    lane = lax.broadcasted_iota(I32, (L, LANES), 1)
    heads_per_tile = LANES // P
    y_tiles = []
    y_off = []
    for g in range(G):
        b_g = xa[:, d_ssd + g * NS:d_ssd + (g + 1) * NS].astype(BF16)
        c_g = xa[:, d_ssd + G * NS + g * NS:d_ssd + G * NS + (g + 1) * NS].astype(BF16)
        cb = _dot_nt(c_g, b_g)
        st = st_ref[g]
        y_off.append(_dot(c_g, st.astype(BF16)))
        st_ref[g] = st * chunk_decay_e[:, g * gw:(g + 1) * gw] + _dot_tn(b_g, xdt_end_b[:, g * gw:(g + 1) * gw])
        for t in range(gw // LANES):
            tile_idx = g * (gw // LANES) + t
            x_tile = xdt_b[:, tile_idx * LANES:(tile_idx + 1) * LANES]
            ms = []
            ws = []
            for j in range(heads_per_tile):
                h = tile_idx * heads_per_tile + j
                seg = a_cs[:, h:h + 1] - a_cs_t[h:h + 1, :]
                dec = jnp.exp(jnp.where(causal, seg, -jnp.inf))
                ms.append((cb * dec).astype(BF16))
                in_head = (lane >= j * P) & (lane < (j + 1) * P)
                ws.append(jnp.where(in_head, x_tile, jnp.zeros_like(x_tile)))
            y_tiles.append(_dot(jnp.concatenate(ms, axis=1), jnp.concatenate(ws, axis=0)))
    y = (jnp.concatenate(y_tiles, axis=1) + jnp.concatenate(y_off, axis=1) * decay_in_e
         + xs * dskip_ref[...])

    gated = y * _silu(z_ref[0])
    outs = []
    for g in range(G):
        outs.append(_rms(gated[:, g * gw:(g + 1) * gw], gn_ref[:, g * gw:(g + 1) * gw]))
    y_ref[0] = jnp.concatenate(outs, axis=1).astype(y_ref.dtype)


def _ssd(xbc, z, dt, conv_w, conv_b, alog, dskip_e, gn, d_ssd, n_heads):
    b, s, ch = xbc.shape
    L = SSD_CHUNK
    tile = lambda w: pl.BlockSpec((1, L, w), lambda i, c: (i, c, 0))
    full = lambda a: pl.BlockSpec(a.shape, lambda i, c: (0, 0))
    return pl.pallas_call(
        functools.partial(_ssd_kernel, d_ssd=d_ssd, n_heads=n_heads),
        grid=(b, s // L),
        in_specs=[tile(ch), tile(d_ssd), tile(LANES), full(conv_w), full(conv_b), full(alog),
                  full(dskip_e), full(gn)],
        out_specs=tile(d_ssd),
        out_shape=jax.ShapeDtypeStruct((b, s, d_ssd), BF16),
        scratch_shapes=[pltpu.VMEM((SUBLANES + L, ch), F32),
                        pltpu.VMEM((SSD_GROUPS, SSD_STATE, d_ssd // SSD_GROUPS), F32)],
        compiler_params=pltpu.CompilerParams(dimension_semantics=("parallel", "arbitrary"),
                                             vmem_limit_bytes=VMEM_LIMIT),
        name="ssd",
    )(xbc, z, dt, conv_w, conv_b, alog, dskip_e, gn)


def _attn_kernel(sink_ref, q_ref, kvc_ref, kvp_ref, an_ref, o_ref, att_ref, *, n_q_heads):
    W = WINDOW
    P = ATTN_HEAD_DIM
    n = pl.program_id(1)
    kv_w = ATTN_KV_HEADS * P
    kv = jnp.concatenate([kvp_ref[0], kvc_ref[0]], axis=0)
    kk = kv[:, :kv_w]
    vv = kv[:, kv_w:]
    swap = lambda x: jnp.concatenate([x[:, P:], x[:, :P]], axis=1)
    kk_s = swap(kk)
    vv_s = swap(vv)
    lane = lax.broadcasted_iota(I32, (2 * W, LANES), 1)
    low = lane < P
    zero = jnp.zeros((2 * W, LANES), kv.dtype)

    def blockdiag(x, x_s, g):
        if g == 0:
            top, bot = jnp.where(low, x, zero), jnp.where(low, zero, x_s)
        else:
            top, bot = jnp.where(low, x_s, zero), jnp.where(low, zero, x)
        return jnp.concatenate([top, bot], axis=0)

    pairs_per_kv = n_q_heads // ATTN_KV_HEADS // 2
    row = lax.broadcasted_iota(I32, (W, W), 0)
    col = lax.broadcasted_iota(I32, (W, W), 1)
    use_cur = row >= col
    live = use_cur | (n > 0)
    lane_w = lax.broadcasted_iota(I32, (W, LANES), 1)
    upper_half = jnp.minimum(lane // P, 1)
    ones_bd = jnp.concatenate([1 - upper_half, upper_half], axis=0).astype(F32).astype(kv.dtype)
    for g in range(ATTN_KV_HEADS):
        k_bd = blockdiag(kk, kk_s, g)
        v_bd = jnp.concatenate([blockdiag(vv, vv_s, g), ones_bd], axis=1)
        hp0 = g * pairs_per_kv
        q_st = jnp.concatenate([q_ref[0, :, (hp0 + p) * LANES:(hp0 + p + 1) * LANES]
                                for p in range(pairs_per_kv)], axis=0)
        s = _dot_nt(q_st, k_bd)
        prob_rows = []
        sink_rows = []
        for p in range(pairs_per_kv):
            pieces = []
            sink_terms = []
            for j in range(2):
                sink = sink_ref[2 * (hp0 + p) + j]
                s_prev = s[p * W:(p + 1) * W, 2 * j * W:(2 * j + 1) * W]
                s_cur = s[p * W:(p + 1) * W, (2 * j + 1) * W:(2 * j + 2) * W]
                s_eff = jnp.where(live, jnp.where(use_cur, s_cur, s_prev), -jnp.inf)
                m = jnp.maximum(jnp.max(s_eff, axis=-1, keepdims=True), sink)
                pj = jnp.exp(s_eff - m)
                p_cur = jnp.where(use_cur, pj, 0.0)
                pieces.append((pj - p_cur).astype(BF16))
                pieces.append(p_cur.astype(BF16))
                sink_terms.append(jnp.broadcast_to(jnp.exp(sink - m), (W, LANES)))
            prob_rows.append(jnp.concatenate(pieces, axis=1))
            sink_rows.append(jnp.where(lane_w < P, sink_terms[0], sink_terms[1]))
        ol = _dot(jnp.concatenate(prob_rows, axis=0), v_bd)
        res = ol[:, :LANES] / (ol[:, LANES:] + jnp.concatenate(sink_rows, axis=0))
        for p in range(pairs_per_kv):
            att_ref[:, (hp0 + p) * LANES:(hp0 + p + 1) * LANES] = res[p * W:(p + 1) * W]
    o_ref[0] = _rms(att_ref[...], an_ref[...]).astype(o_ref.dtype)


def _attn(sinks, q, kv, an, n_q_heads):
    b, s, dq = q.shape
    W = WINDOW
    grid_spec = pltpu.PrefetchScalarGridSpec(
        num_scalar_prefetch=1,
        grid=(b, s // W),
        in_specs=[pl.BlockSpec((1, W, dq), lambda i, n, sk: (i, n, 0)),
                  pl.BlockSpec((1, W, kv.shape[2]), lambda i, n, sk: (i, n, 0)),
                  pl.BlockSpec((1, W, kv.shape[2]), lambda i, n, sk: (i, jnp.maximum(n - 1, 0), 0)),
                  pl.BlockSpec(an.shape, lambda i, n, sk: (0, 0))],
        out_specs=pl.BlockSpec((1, W, dq), lambda i, n, sk: (i, n, 0)),
        scratch_shapes=[pltpu.VMEM((W, dq), F32)],
    )
    return pl.pallas_call(
        functools.partial(_attn_kernel, n_q_heads=n_q_heads),
        grid_spec=grid_spec,
        out_shape=jax.ShapeDtypeStruct((b, s, dq), BF16),
        compiler_params=pltpu.CompilerParams(dimension_semantics=("parallel", "arbitrary"),
                                             vmem_limit_bytes=VMEM_LIMIT),
        name="attn",
    )(sinks, q, kv, kv, an)


def _outproj_kernel(x_ref, ys_ref, ya_ref, wos_ref, woa_ref, nf_ref, wr_ref, tri_ref,
                    h1_ref, h2_ref, ri_ref, gt_ref, cnt_ref, base_ref):
    tm = x_ref.shape[0]
    G = MOE_GROUPS
    E = EXPERTS_PER_GROUP
    NE = G * E

    @pl.when(pl.program_id(0) == 0)
    def _():
        base_ref[...] = jnp.zeros(base_ref.shape, F32)

    h1 = x_ref[...] + _dot(ys_ref[...], wos_ref[...]) + _dot(ya_ref[...], woa_ref[...])
    h1_ref[...] = h1
    h2 = _rms(h1, nf_ref[...])
    h2_ref[...] = h2
    logit_t = _dot_nt(wr_ref[...], h2.astype(BF16))
    gl = logit_t[0:G]
    sub = lax.broadcasted_iota(I32, (G, tm), 0)
    gmax = jnp.max(gl, axis=0, keepdims=True)
    g_sel = jnp.min(jnp.where(gl == gmax, sub, G), axis=0, keepdims=True)
    p_group = 1.0 / jnp.sum(jnp.exp(gl - gmax), axis=0, keepdims=True)
    in_group = jnp.zeros((E, tm), F32)
    for g in range(G):
        in_group = jnp.where(g_sel == g, logit_t[G + g * E:G + (g + 1) * E], in_group)
    t1 = jnp.max(in_group, axis=0, keepdims=True)
    i1 = jnp.min(jnp.where(in_group == t1, sub, E), axis=0, keepdims=True)
    rest = jnp.where(sub == i1, -jnp.inf, in_group)
    t2 = jnp.max(rest, axis=0, keepdims=True)
    i2 = jnp.min(jnp.where(rest == t2, sub, E), axis=0, keepdims=True)
    e2 = jnp.exp(t2 - t1)
    gate1 = p_group / (1.0 + e2)
    gate2 = p_group * e2 / (1.0 + e2)
    eid1 = g_sel * E + i1
    eid2 = g_sel * E + i2

    e_iota = lax.broadcasted_iota(I32, (NE, tm), 0)
    hot1 = e_iota == eid1
    hot2 = e_iota == eid2
    two_hot = jnp.where(hot1 | hot2, 1.0, 0.0).astype(BF16)
    pref = _dot(two_hot, tri_ref[...])
    base = base_ref[...]
    before = pref[:, :tm] + jnp.concatenate([base] * (tm // LANES), axis=1)
    rank1 = jnp.sum(jnp.where(hot1, before, 0.0), axis=0, keepdims=True)
    rank2 = jnp.sum(jnp.where(hot2, before, 0.0), axis=0, keepdims=True)
    base = base + pref[:, tm:]
    base_ref[...] = base
    cnt_ref[...] = base

    ri_ref[...] = jnp.zeros(ri_ref.shape, I32)
    ri_ref[0:1, :] = eid1
    ri_ref[1:2, :] = eid2
    ri_ref[2:3, :] = rank1.astype(I32)
    ri_ref[3:4, :] = rank2.astype(I32)
    gates_t = jnp.concatenate([gate1, gate2, jnp.zeros((LANES - 2, tm), F32)], axis=0)
    for c in range(tm // LANES):
        gt_ref[c * LANES:(c + 1) * LANES, :] = gates_t[:, c * LANES:(c + 1) * LANES].T


def _outproj(x2, ys, ya, wos, woa, nf, wr_t, tri):
    n, d = x2.shape
    tm = TOKEN_TILE
    ne = MOE_GROUPS * EXPERTS_PER_GROUP
    row = lambda w: pl.BlockSpec((tm, w), lambda i: (i, 0))
    full = lambda a: pl.BlockSpec(a.shape, lambda i: (0, 0))
    return pl.pallas_call(
        _outproj_kernel,
        grid=(n // tm,),
        in_specs=[row(d), row(ys.shape[1]), row(ya.shape[1]), full(wos), full(woa), full(nf),
                  full(wr_t), full(tri)],
        out_specs=[row(d), row(d), pl.BlockSpec((SUBLANES, tm), lambda i: (0, i)), row(LANES),
                   pl.BlockSpec((ne, LANES), lambda i: (0, 0))],
        out_shape=[jax.ShapeDtypeStruct((n, d), F32), jax.ShapeDtypeStruct((n, d), F32),
                   jax.ShapeDtypeStruct((SUBLANES, n), I32), jax.ShapeDtypeStruct((n, LANES), F32),
                   jax.ShapeDtypeStruct((ne, LANES), F32)],
        scratch_shapes=[pltpu.VMEM((ne, LANES), F32)],
        compiler_params=pltpu.CompilerParams(dimension_semantics=("arbitrary",),
                                             vmem_limit_bytes=VMEM_LIMIT),
        name="outproj_router",
    )(x2, ys, ya, wos, woa, nf, wr_t, tri)


def _dest_kernel(ri_ref, ps_ref, d_ref):
    ne = ps_ref.shape[0]
    tm = ri_ref.shape[1]
    e_iota = lax.broadcasted_iota(I32, (ne, tm), 0)
    ps = jnp.concatenate([ps_ref[...]] * (tm // LANES), axis=1)
    d_ref[...] = jnp.zeros(d_ref.shape, I32)
    for k in range(TOP_K):
        start = jnp.sum(jnp.where(e_iota == ri_ref[k:k + 1, :], ps, 0), axis=0, keepdims=True)
        d_ref[k:k + 1, :] = start + ri_ref[TOP_K + k:TOP_K + k + 1, :]


def _dest(route_i, pstart_rep):
    n = route_i.shape[1]
    tm = min(2048, n)
    return pl.pallas_call(
        _dest_kernel,
        grid=(n // tm,),
        in_specs=[pl.BlockSpec((SUBLANES, tm), lambda i: (0, i)),
                  pl.BlockSpec(pstart_rep.shape, lambda i: (0, 0))],
        out_specs=pl.BlockSpec((SUBLANES, tm), lambda i: (0, i)),
        out_shape=jax.ShapeDtypeStruct((SUBLANES, n), I32),
        compiler_params=pltpu.CompilerParams(dimension_semantics=("parallel",)),
        name="dest_rows",
    )(route_i, pstart_rep)


def _dispatch_kernel(dest_ref, h_ref, buf_in_ref, buf_ref, sem, *, n_tokens):
    del buf_in_ref
    tm = h_ref.shape[0]
    base = pl.program_id(0) * tm

    def issue(t, carry):
        for k in range(TOP_K):
            d = dest_ref[k * n_tokens + base + t]
            pltpu.make_async_copy(h_ref.at[pl.ds(t, 1)], buf_ref.at[pl.ds(d, 1)], sem).start()
        return carry

    lax.fori_loop(0, tm, issue, 0, unroll=DMA_UNROLL)
    for k in range(TOP_K):
        pltpu.make_async_copy(h_ref, buf_ref.at[pl.ds(0, tm)], sem).wait()


def _dispatch(dest_flat, h2, buf0):
    n, d = h2.shape
    tm = ROW_TILE
    grid_spec = pltpu.PrefetchScalarGridSpec(
        num_scalar_prefetch=1,
        grid=(n // tm,),
        in_specs=[pl.BlockSpec((tm, d), lambda i, ds: (i, 0)),
                  pl.BlockSpec(memory_space=pl.ANY)],
        out_specs=pl.BlockSpec(memory_space=pl.ANY),
        scratch_shapes=[pltpu.SemaphoreType.DMA(())],
    )
    return pl.pallas_call(
        functools.partial(_dispatch_kernel, n_tokens=n),
        grid_spec=grid_spec,
        out_shape=jax.ShapeDtypeStruct(buf0.shape, buf0.dtype),
        input_output_aliases={2: 0},
        compiler_params=pltpu.CompilerParams(dimension_semantics=("arbitrary",)),
        name="dispatch",
    )(dest_flat, h2, buf0)


def _combine_kernel(dest_ref, ybuf_ref, h1_ref, gt_ref, nfin_ref, o_ref, rows_ref, sem, *, n_tokens):
    tm = h1_ref.shape[0]
    base = pl.program_id(0) * tm

    def issue(t, carry):
        for k in range(TOP_K):
            d = dest_ref[k * n_tokens + base + t]
            pltpu.make_async_copy(ybuf_ref.at[pl.ds(d, 1)], rows_ref.at[k, pl.ds(t, 1)], sem).start()
        return carry

    lax.fori_loop(0, tm, issue, 0, unroll=DMA_UNROLL)
    for k in range(TOP_K):
        pltpu.make_async_copy(ybuf_ref.at[pl.ds(0, tm)], rows_ref.at[k], sem).wait()

    y = jnp.zeros(h1_ref.shape, F32)
    for k in range(TOP_K):
        y = y + rows_ref[k] * gt_ref[:, k:k + 1]
    o_ref[...] = _rms(h1_ref[...] + y, nfin_ref[...])


def _combine(dest_flat, ybuf, h1, gates_tm, nfin):
    n, d = h1.shape
    tm = ROW_TILE
    grid_spec = pltpu.PrefetchScalarGridSpec(
        num_scalar_prefetch=1,
        grid=(n // tm,),
        in_specs=[pl.BlockSpec(memory_space=pl.ANY),
                  pl.BlockSpec((tm, d), lambda i, ds: (i, 0)),
                  pl.BlockSpec((tm, LANES), lambda i, ds: (i, 0)),
                  pl.BlockSpec(nfin.shape, lambda i, ds: (0, 0))],
        out_specs=pl.BlockSpec((tm, d), lambda i, ds: (i, 0)),
        scratch_shapes=[pltpu.VMEM((TOP_K, tm, d), F32), pltpu.SemaphoreType.DMA(())],
    )
    return pl.pallas_call(
        functools.partial(_combine_kernel, n_tokens=n),
        grid_spec=grid_spec,
        out_shape=jax.ShapeDtypeStruct((n, d), F32),
        compiler_params=pltpu.CompilerParams(dimension_semantics=("arbitrary",),
                                             vmem_limit_bytes=VMEM_LIMIT),
        name="combine",
    )(dest_flat, ybuf, h1, gates_tm, nfin)


def _expert_kernel(be_ref, nb_ref, x_ref, wg_ref, wu_ref, wd_ref, y_ref):
    del be_ref
    used = pl.program_id(0) < nb_ref[0]

    @pl.when(used)
    def _():
        xb = x_ref[...].astype(BF16)
        gate = _dot(xb, wg_ref[0].astype(BF16))
        up = _dot(xb, wu_ref[0].astype(BF16))
        hid = (_silu(gate) * up).astype(BF16)
        y_ref[...] = _dot(hid, wd_ref[0].astype(BF16))

    @pl.when(jnp.logical_not(used))
    def _():
        y_ref[...] = jnp.zeros(y_ref.shape, F32)


def _experts(block_expert, n_used, buf, w_gate, w_up, w_down):
    rows, d = buf.shape
    blk = EXPERT_BLOCK
    f = w_gate.shape[2]
    grid_spec = pltpu.PrefetchScalarGridSpec(
        num_scalar_prefetch=2,
        grid=(rows // blk,),
        in_specs=[pl.BlockSpec((blk, d), lambda i, be, nb: (jnp.minimum(i, nb[0] - 1), 0)),
                  pl.BlockSpec((1, d, f), lambda i, be, nb: (be[i], 0, 0)),
                  pl.BlockSpec((1, d, f), lambda i, be, nb: (be[i], 0, 0)),
                  pl.BlockSpec((1, f, d), lambda i, be, nb: (be[i], 0, 0))],
        out_specs=pl.BlockSpec((blk, d), lambda i, be, nb: (i, 0)),
    )
    return pl.pallas_call(
        _expert_kernel,
        grid_spec=grid_spec,
        out_shape=jax.ShapeDtypeStruct((rows, d), F32),
        compiler_params=pltpu.CompilerParams(dimension_semantics=("arbitrary",),
                                             vmem_limit_bytes=VMEM_LIMIT),
        name="experts",
    )(block_expert, n_used, buf, w_gate, w_up, w_down)


def _layer(h, norm_mix, w_in, conv_w, conv_b, dt_bias, a_log, d_skip, ssd_norm, attn_sinks,
           attn_norm, w_out, norm_ffn, w_rg, w_re, w_gate, w_up, w_down, final_gain):
    b, s, d = h.shape
    n = b * s
    n_heads = dt_bias.shape[0]
    d_ssd = n_heads * SSD_HEAD_DIM
    conv_ch = conv_w.shape[1]
    n_q_heads = attn_sinks.shape[0]
    d_attn = n_q_heads * ATTN_HEAD_DIM
    ne = MOE_GROUPS * EXPERTS_PER_GROUP
    row = lambda v: v.reshape(1, -1)
    pad_lanes = lambda v: jnp.pad(v, ((0, 0), (0, LANES - v.shape[1])))

    o1, o2, o3, o4 = d_ssd, d_ssd + conv_ch, d_ssd + conv_ch + n_heads, d_ssd + conv_ch + n_heads + d_attn
    wb = w_in.astype(BF16)
    x2 = h.reshape(n, d)
    z, xbc, dt, q, kv = _inproj(
        x2, row(norm_mix), wb[:, :o1], wb[:, o1:o2], pad_lanes(wb[:, o2:o3]), wb[:, o3:o4], wb[:, o4:],
        pad_lanes(row(dt_bias)), ATTN_HEAD_DIM ** -0.5)

    y_ssd = _ssd(xbc.reshape(b, s, conv_ch), z.reshape(b, s, d_ssd), dt.reshape(b, s, LANES),
                 conv_w, row(conv_b), pad_lanes(row(a_log)), row(jnp.repeat(d_skip, SSD_HEAD_DIM)),
                 row(ssd_norm), d_ssd, n_heads)
    y_att = _attn(attn_sinks, q.reshape(b, s, d_attn), kv.reshape(b, s, kv.shape[1]), row(attn_norm),
                  n_q_heads)

    wo = w_out.astype(BF16)
    wr_t = jnp.pad(jnp.concatenate([w_rg, w_re], axis=1).T, ((0, LANES - MOE_GROUPS - ne), (0, 0))).astype(BF16)
    tm = TOKEN_TILE
    tri = jnp.concatenate([jnp.triu(jnp.ones((tm, tm), BF16), 1), jnp.ones((tm, LANES), BF16)], axis=1)
    h1, h2, route_i, gates_tm, counts = _outproj(
        x2, y_ssd.reshape(n, d_ssd), y_att.reshape(n, d_attn), wo[:d_ssd], wo[d_ssd:], row(norm_ffn),
        wr_t, tri)

    blk = EXPERT_BLOCK
    cnt = counts[:, 0].astype(I32)
    padded = (cnt + blk - 1) // blk * blk
    pend = jnp.cumsum(padded)
    pstart = pend - padded
    n_blocks = -(-(n * TOP_K + ne * (blk - 1)) // blk)
    n_used = (pend[-1] // blk).astype(I32)
    blk_start = jnp.arange(n_blocks, dtype=I32) * blk
    blk_e = jnp.minimum(jnp.sum((pend[None, :] <= blk_start[:, None]).astype(I32), axis=1), ne - 1)
    last_e = jnp.max(jnp.where(cnt > 0, jnp.arange(ne, dtype=I32), 0))
    blk_e = jnp.where(jnp.arange(n_blocks) < n_used, blk_e, last_e).astype(I32)

    dest = _dest(route_i, jnp.broadcast_to(pstart[:, None], (ne, LANES)).astype(I32))
    dest_flat = dest[:TOP_K].reshape(-1)
    buf = _dispatch(dest_flat, h2, jnp.zeros((n_blocks * blk, d), F32))
    ybuf = _experts(blk_e, n_used.reshape(1), buf, w_gate, w_up, w_down)
    out = _combine(dest_flat, ybuf, h1, gates_tm, row(final_gain))
    return out.reshape(b, s, d)


def kernel(x, norm_mix, w_in, conv_w, conv_b, dt_bias, a_log, d_skip, ssd_norm, attn_sinks, attn_norm,
           w_out, norm_ffn, w_router_group, w_router_expert, w_gate, w_up, w_down, norm_final):
    depth = w_in.shape[0]
    assert depth == 1, "the fused final rmsnorm assumes a single layer"
    return _layer(x, norm_mix[0], w_in[0], conv_w[0], conv_b[0], dt_bias[0], a_log[0], d_skip[0],
                  ssd_norm[0], attn_sinks[0], attn_norm[0], w_out[0], norm_ffn[0], w_router_group[0],
                  w_router_expert[0], w_gate[0], w_up[0], w_down[0], norm_final)
```
